```python
import functools
import jax, jax.numpy as jnp
from jax import lax
import numpy as np

D_MODEL = 4096
BATCH = 2
SEQ = 4096
DEPTH = 1
DEC_BATCH = 128
DEC_SEQ = 1
PAST_LEN = 2048
PAGE_SIZE = 128

N_HEADS = 16
HEAD_DIM = 128
ATTN_WIDTH = N_HEADS * HEAD_DIM
MOBA_BLOCK = 256
MOBA_TOPK = 3
Q_CHUNK = 64
ROPE_THETA = 10000.0
POOL_WIDTH = D_MODEL // 2
POOL_WINDOWS = (2, 4, 8, 16)
POOL_GROUPS = len(POOL_WINDOWS)
POOL_GROUP_WIDTH = POOL_WIDTH // POOL_GROUPS
POOL_STATE = max(POOL_WINDOWS) - 1
IN_WIDTH = 3 * ATTN_WIDTH + POOL_WIDTH
PEER_HEADS = 8
PEER_KEYS = 128
PEER_EXPERTS = PEER_KEYS * PEER_KEYS
PEER_TOPK = 16
PEER_DK = 256
PEER_BLOCK = 64
PLE_DIM = 256
EPS = 1e-6

kernel_name = "moba_pool_peer_hybrid_step"


def rmsnorm(x, g):
    xf = x.astype(jnp.float32)
    inv = lax.rsqrt(jnp.mean(xf * xf, axis=-1, keepdims=True) + EPS)
    return (xf * inv * g.astype(jnp.float32)).astype(x.dtype)


def rope(x, pos):
    half = HEAD_DIM // 2
    freqs = ROPE_THETA ** (-jnp.arange(half, dtype=jnp.float32) / half)
    ang = pos.astype(jnp.float32)[:, None] * freqs[None, :]
    cos = jnp.cos(ang)[:, None, :]
    sin = jnp.sin(ang)[:, None, :]
    xf = x.astype(jnp.float32)
    x1, x2 = xf[..., :half], xf[..., half:]
    return jnp.concatenate([x1 * cos - x2 * sin, x2 * cos + x1 * sin], axis=-1).astype(x.dtype)


def mixer_inputs(h, w_in, q_norm, k_norm, pos):
    b, s, _ = h.shape
    z = h @ w_in
    q, k, v, u = jnp.split(z, [ATTN_WIDTH, 2 * ATTN_WIDTH, 3 * ATTN_WIDTH], axis=-1)
    q = rope(rmsnorm(q.reshape(b, s, N_HEADS, HEAD_DIM), q_norm), pos)
    k = rope(rmsnorm(k.reshape(b, s, N_HEADS, HEAD_DIM), k_norm), pos)
    v = v.reshape(b, s, N_HEADS, HEAD_DIM)
    return q, k, v, u


def select_blocks(gate, n_sel):
    if n_sel == 0:
        return jnp.zeros(gate.shape[:-1] + (0,), jnp.int32)
    return lax.top_k(gate, n_sel)[1]


def attend_two_part(q, k_sel, v_sel, sel_mask, k_loc, v_loc, loc_mask):
    scale = HEAD_DIM ** -0.5
    s_sel = jnp.einsum("bhtd,bhtnd->bhtn", q, k_sel, preferred_element_type=jnp.float32) * scale
    s_loc = jnp.einsum("bhtd,bhld->bhtl", q, k_loc, preferred_element_type=jnp.float32) * scale
    s = jnp.concatenate([jnp.where(sel_mask, s_sel, -jnp.inf), jnp.where(loc_mask, s_loc, -jnp.inf)], axis=-1)
    p = jax.nn.softmax(s, axis=-1)
    n = k_sel.shape[3]
    out = jnp.einsum("bhtn,bhtnd->bhtd", p[..., :n].astype(v_sel.dtype), v_sel)
    return out + jnp.einsum("bhtl,bhld->bhtd", p[..., n:].astype(v_loc.dtype), v_loc)


def moba_prompt(q, k, v):
    b, s = q.shape[:2]
    nb = -(-s // MOBA_BLOCK)
    s_pad = nb * MOBA_BLOCK
    nc = s_pad // Q_CHUNK
    pad = ((0, 0), (0, s_pad - s), (0, 0), (0, 0))
    qh = jnp.pad(q, pad).transpose(0, 2, 1, 3)
    kb = jnp.pad(k, pad).transpose(0, 2, 1, 3).reshape(b, N_HEADS, nb, MOBA_BLOCK, HEAD_DIM)
    vb = jnp.pad(v, pad).transpose(0, 2, 1, 3).reshape(b, N_HEADS, nb, MOBA_BLOCK, HEAD_DIM)
    k_mean = jnp.mean(kb, axis=3, dtype=jnp.float32)
    own = jnp.arange(s_pad) // MOBA_BLOCK
    gate = jnp.einsum("bhsd,bhnd->bhsn", qh.astype(jnp.float32), k_mean)
    gate = jnp.where(jnp.arange(nb)[None, :] < own[:, None], gate, -jnp.inf)
    n_sel = min(MOBA_TOPK, nb - 1)
    sel = select_blocks(gate, n_sel)
    valid = jnp.arange(n_sel)[None, :] < own[:, None]
    q_chunks = qh.reshape(b, N_HEADS, nc, Q_CHUNK, HEAD_DIM).transpose(2, 0, 1, 3, 4)
    sel_chunks = sel.reshape(b, N_HEADS, nc, Q_CHUNK, n_sel).transpose(2, 0, 1, 3, 4)
    valid_chunks = valid.reshape(nc, Q_CHUNK, n_sel)
    bi = jnp.arange(b)[:, None, None, None]
    hi = jnp.arange(N_HEADS)[None, :, None, None]

    def one_chunk(args):
        c, q_c, sel_c, valid_c = args
        q0 = c * Q_CHUNK
        ob = q0 // MOBA_BLOCK
        k_sel = kb[bi, hi, sel_c].reshape(b, N_HEADS, Q_CHUNK, n_sel * MOBA_BLOCK, HEAD_DIM)
        v_sel = vb[bi, hi, sel_c].reshape(b, N_HEADS, Q_CHUNK, n_sel * MOBA_BLOCK, HEAD_DIM)
        sel_mask = jnp.repeat(valid_c, MOBA_BLOCK, axis=-1)
        k_loc = lax.dynamic_index_in_dim(kb, ob, axis=2, keepdims=False)
        v_loc = lax.dynamic_index_in_dim(vb, ob, axis=2, keepdims=False)
        q_pos = q0 + jnp.arange(Q_CHUNK)
        k_pos = ob * MOBA_BLOCK + jnp.arange(MOBA_BLOCK)
        loc_mask = k_pos[None, :] <= q_pos[:, None]
        return attend_two_part(q_c, k_sel, v_sel, sel_mask, k_loc, v_loc, loc_mask)

    out = lax.map(one_chunk, (jnp.arange(nc), q_chunks, sel_chunks, valid_chunks))
    out = out.transpose(1, 0, 3, 2, 4).reshape(b, s_pad, ATTN_WIDTH)
    return out[:, :s]


def moba_sample(q, k_new, v_new, page_sums, cache_k, cache_v, layer, page_table):
    db, t = q.shape[:2]
    ppb = MOBA_BLOCK // PAGE_SIZE
    nbp = PAST_LEN // MOBA_BLOCK
    tail = PAST_LEN - nbp * MOBA_BLOCK
    assert tail + t <= MOBA_BLOCK, "new tokens must lie in one MoBA block"
    qh = q.transpose(0, 2, 1, 3)
    blk_sums = page_sums[page_table[:, : nbp * ppb]].reshape(db, nbp, ppb, N_HEADS, HEAD_DIM).sum(axis=2)
    k_mean = blk_sums / MOBA_BLOCK
    n_sel = min(MOBA_TOPK, nbp)
    sel = select_blocks(jnp.einsum("bhtd,bnhd->bhtn", qh.astype(jnp.float32), k_mean), n_sel)
    logical = sel[..., None] * ppb + jnp.arange(ppb)
    phys = page_table[jnp.arange(db)[:, None, None, None, None], logical]
    hi = jnp.arange(N_HEADS)[None, :, None, None, None]
    k_sel = cache_k[layer, phys, :, hi].reshape(db, N_HEADS, t, n_sel * MOBA_BLOCK, HEAD_DIM)
    v_sel = cache_v[layer, phys, :, hi].reshape(db, N_HEADS, t, n_sel * MOBA_BLOCK, HEAD_DIM)
    sel_mask = jnp.ones((t, n_sel * MOBA_BLOCK), bool)
    tail_pages = page_table[:, nbp * ppb:]
    k_tail = cache_k[layer, tail_pages].reshape(db, tail, N_HEADS, HEAD_DIM)
    v_tail = cache_v[layer, tail_pages].reshape(db, tail, N_HEADS, HEAD_DIM)
    k_loc = jnp.concatenate([k_tail, k_new], axis=1).transpose(0, 2, 1, 3)
    v_loc = jnp.concatenate([v_tail, v_new], axis=1).transpose(0, 2, 1, 3)
    loc_mask = jnp.arange(tail + t)[None, :] <= (tail + jnp.arange(t))[:, None]
    out = attend_two_part(qh, k_sel, v_sel, sel_mask, k_loc, v_loc, loc_mask)
    return out.transpose(0, 2, 1, 3).reshape(db, t, ATTN_WIDTH)


def pool_mix(u, state_rows, pool_w, pool_scale):
    b, s, c = u.shape
    n0 = state_rows.shape[1]
    xcat = jnp.concatenate([state_rows, u], axis=1)
    L = xcat.shape[1]
    cs = jnp.concatenate([jnp.zeros((b, 1, c), jnp.float32), jnp.cumsum(xcat.astype(jnp.float32), axis=1)], axis=1)
    win = jnp.repeat(jnp.array(POOL_WINDOWS, jnp.int32), POOL_GROUP_WIDTH)
    hi = jnp.arange(n0, L) + 1
    lo = jnp.maximum(hi[:, None] - win[None, :], 0)
    sums = cs[:, n0 + 1:] - jnp.take_along_axis(cs, jnp.broadcast_to(lo[None], (b, s, c)), axis=1)
    cnt = (hi[:, None] - lo).astype(jnp.float32)
    pooled = (sums / cnt - u.astype(jnp.float32)).astype(u.dtype)
    mixed = jnp.einsum("bsgc,gcd->bsgd", pooled.reshape(b, s, POOL_GROUPS, POOL_GROUP_WIDTH), pool_w)
    return mixed.reshape(b, s, c) * pool_scale, xcat[:, L - POOL_STATE:]


def peer_ffn(h, wq, subkeys, u_tab, v_tab):
    shp = h.shape
    flat = h.reshape(-1, D_MODEL)
    n = flat.shape[0]
    nblk = -(-n // PEER_BLOCK)
    flat = jnp.pad(flat, ((0, nblk * PEER_BLOCK - n), (0, 0))).reshape(nblk, PEER_BLOCK, D_MODEL)

    def one_block(hb):
        qh = (hb @ wq).reshape(PEER_BLOCK, PEER_HEADS, 2, PEER_DK // 2)
        s = jnp.einsum("nhcd,hckd->nhck", qh, subkeys, preferred_element_type=jnp.float32)
        s1, i1 = lax.top_k(s[:, :, 0], PEER_TOPK)
        s2, i2 = lax.top_k(s[:, :, 1], PEER_TOPK)
        cand = (s1[..., :, None] + s2[..., None, :]).reshape(PEER_BLOCK, PEER_HEADS, PEER_TOPK * PEER_TOPK)
        top_s, top_c = lax.top_k(cand, PEER_TOPK)
        e1 = jnp.take_along_axis(i1, top_c // PEER_TOPK, axis=-1)
        e2 = jnp.take_along_axis(i2, top_c % PEER_TOPK, axis=-1)
        experts = e1 * PEER_KEYS + e2
        g = jax.nn.softmax(top_s, axis=-1)
        a = jnp.einsum("nd,nhkd->nhk", hb, u_tab[experts], preferred_element_type=jnp.float32)
        w = (g * jax.nn.gelu(a, approximate=False)).astype(hb.dtype)
        return jnp.einsum("nhk,nhkd->nd", w, v_tab[experts])

    out = lax.map(one_block, flat).reshape(-1, D_MODEL)[:n]
    return out.reshape(shp)


def trunk_layer(x, p_emb, pos, attend, pool_prev, norm_mix, w_in, q_norm, k_norm, pool_w, pool_scale,
                w_up_attn, w_up_pool, w_gate, b_gate, w_out, norm_ffn, peer_wq, peer_subkeys,
                peer_u, peer_v, ple_proj, ple_norm, ple_gate):
    h = rmsnorm(x, norm_mix)
    q, k, v, u = mixer_inputs(h, w_in, q_norm, k_norm, pos)
    y_attn = attend(q, k, v)
    y_pool, pool_new = pool_mix(u, pool_prev, pool_w, pool_scale)
    g_attn, g_pool = jnp.split(jax.nn.sigmoid(h @ w_gate + b_gate), 2, axis=-1)
    merged = g_attn * (y_attn @ w_up_attn) + g_pool * (y_pool @ w_up_pool)
    x = x + merged @ w_out
    x = x + peer_ffn(rmsnorm(x, norm_ffn), peer_wq, peer_subkeys, peer_u, peer_v)
    x = x + jax.nn.sigmoid(rmsnorm(x, ple_norm) @ ple_gate) * (p_emb @ ple_proj)
    return x, k, v, pool_new


def setup_inputs(seed: int = 0) -> dict:
    key = jax.random.key(seed)
    ks = jax.random.split(key, 32)
    f32 = jnp.float32
    n_pages = PAST_LEN // PAGE_SIZE
    n_phys = (DEC_BATCH * n_pages * 5) // 4

    def nrm(k, shape, scale):
        return jax.random.normal(k, shape, f32) * scale

    def gain(k, shape):
        return 1.0 + 0.02 * jax.random.normal(k, shape, f32)

    page_table = jax.random.permutation(ks[5], n_phys)[: DEC_BATCH * n_pages].reshape(DEC_BATCH, n_pages).astype(jnp.int32)
    return {
        "x_prompt": nrm(ks[0], (BATCH, SEQ, D_MODEL), 1.0),
        "x_sample": nrm(ks[1], (DEC_BATCH, DEC_SEQ, D_MODEL), 1.0),
        "cache_k": nrm(ks[2], (DEPTH, n_phys, PAGE_SIZE, N_HEADS, HEAD_DIM), 1.0),
        "cache_v": nrm(ks[3], (DEPTH, n_phys, PAGE_SIZE, N_HEADS, HEAD_DIM), 1.0),
        "state_pool": nrm(ks[4], (DEPTH, DEC_BATCH, POOL_STATE, POOL_WIDTH), 1.0),
        "page_table": page_table,
        "p_prompt": nrm(ks[6], (DEPTH, BATCH, SEQ, PLE_DIM), 1.0),
        "p_sample": nrm(ks[7], (DEPTH, DEC_BATCH, DEC_SEQ, PLE_DIM), 1.0),
        "norm_mix": gain(ks[8], (DEPTH, D_MODEL)),
        "w_in": nrm(ks[9], (DEPTH, D_MODEL, IN_WIDTH), D_MODEL ** -0.5),
        "q_norm": gain(ks[10], (DEPTH, HEAD_DIM)),
        "k_norm": gain(ks[11], (DEPTH, HEAD_DIM)),
        "pool_w": nrm(ks[12], (DEPTH, POOL_GROUPS, POOL_GROUP_WIDTH, POOL_GROUP_WIDTH), POOL_GROUP_WIDTH ** -0.5),
        "pool_scale": gain(ks[13], (DEPTH, POOL_WIDTH)),
        "w_up_attn": nrm(ks[14], (DEPTH, ATTN_WIDTH, D_MODEL), ATTN_WIDTH ** -0.5),
        "w_up_pool": nrm(ks[15], (DEPTH, POOL_WIDTH, D_MODEL), POOL_WIDTH ** -0.5),
        "w_gate": nrm(ks[16], (DEPTH, D_MODEL, 2 * D_MODEL), D_MODEL ** -0.5),
        "b_gate": nrm(ks[17], (DEPTH, 2 * D_MODEL), 0.01),
        "w_out": nrm(ks[18], (DEPTH, D_MODEL, D_MODEL), D_MODEL ** -0.5),
        "norm_ffn": gain(ks[19], (DEPTH, D_MODEL)),
        "peer_wq": nrm(ks[20], (DEPTH, D_MODEL, PEER_HEADS * PEER_DK), D_MODEL ** -0.5),
        "peer_subkeys": nrm(ks[21], (DEPTH, PEER_HEADS, 2, PEER_KEYS, PEER_DK // 2), (PEER_DK // 2) ** -0.5),
        "peer_u": nrm(ks[22], (DEPTH, PEER_EXPERTS, D_MODEL), D_MODEL ** -0.5),
        "peer_v": nrm(ks[23], (DEPTH, PEER_EXPERTS, D_MODEL), PEER_HEADS ** -0.5),
        "ple_proj": nrm(ks[24], (DEPTH, PLE_DIM, D_MODEL), PLE_DIM ** -0.5),
        "ple_norm": gain(ks[25], (DEPTH, D_MODEL)),
        "ple_gate": nrm(ks[26], (DEPTH, D_MODEL, D_MODEL), D_MODEL ** -0.5),
    }


def reference(x_prompt, x_sample, cache_k, cache_v, state_pool, page_table, p_prompt, p_sample,
              norm_mix, w_in, q_norm, k_norm, pool_w, pool_scale, w_up_attn, w_up_pool, w_gate, b_gate,
              w_out, norm_ffn, peer_wq, peer_subkeys, peer_u, peer_v, ple_proj, ple_norm, ple_gate):
    pos_p = jnp.arange(x_prompt.shape[1], dtype=jnp.int32)
    pos_s = PAST_LEN + jnp.arange(x_sample.shape[1], dtype=jnp.int32)
    page_sums = jnp.sum(cache_k, axis=2, dtype=jnp.float32)
    yp, ys = x_prompt, x_sample
    kp_l, vp_l, pp_l, ks_l, vs_l, ps_l = [], [], [], [], [], []
    for l in range(DEPTH):
        lw = [w[l] for w in (norm_mix, w_in, q_norm, k_norm, pool_w, pool_scale, w_up_attn, w_up_pool,
                             w_gate, b_gate, w_out, norm_ffn, peer_wq, peer_subkeys, peer_u, peer_v,
                             ple_proj, ple_norm, ple_gate)]
        empty_pool = jnp.zeros((yp.shape[0], 0, POOL_WIDTH), yp.dtype)
        yp, kp, vp, pp = trunk_layer(yp, p_prompt[l], pos_p, moba_prompt, empty_pool, *lw)
        attend_s = functools.partial(moba_sample, page_sums=page_sums[l], cache_k=cache_k, cache_v=cache_v,
                                     layer=l, page_table=page_table)
        ys, ks, vs, ps = trunk_layer(ys, p_sample[l], pos_s, attend_s, state_pool[l], *lw)
        kp_l.append(kp); vp_l.append(vp); pp_l.append(pp)
        ks_l.append(ks); vs_l.append(vs); ps_l.append(ps)
    return (yp, ys, jnp.stack(kp_l), jnp.stack(vp_l), jnp.stack(pp_l), jnp.stack(ks_l), jnp.stack(vs_l), jnp.stack(ps_l))
```

```python
import functools
import math

import jax
import jax.numpy as jnp
from jax import lax
from jax.experimental import pallas as pl
from jax.experimental.pallas import tpu as pltpu

F32 = jnp.float32
BF16 = jnp.bfloat16
I32 = jnp.int32

EPS = 1e-6
MOBA_BLOCK = 256
MOBA_TOPK = 3
ROPE_THETA = 10000.0
POOL_WINDOWS = (2, 4, 8, 16)
PEER_TOPK = 16

LANES = 128
MIB = 2 ** 20
NT_DIMS = (((1,), (1,)), ((), ()))
NEG_INF = float("-inf")


def _params(semantics, vmem_mib):
    return pltpu.CompilerParams(dimension_semantics=semantics, vmem_limit_bytes=vmem_mib * MIB)


def _pick(n, *candidates):
    for c in candidates:
        if n % c == 0:
            return c
    return n


def _rmsnorm_body(x_ref, g_ref, o_ref):
    x = x_ref[...]
    inv = lax.rsqrt(jnp.mean(x * x, axis=-1, keepdims=True) + EPS)
    o_ref[...] = (x * inv * g_ref[...]).astype(o_ref.dtype)


def _rmsnorm(x, g):
    m, d = x.shape
    tm = _pick(m, 256, 128)
    return pl.pallas_call(
        _rmsnorm_body,
        out_shape=jax.ShapeDtypeStruct((m, d), BF16),
        grid=(m // tm,),
        in_specs=[pl.BlockSpec((tm, d), lambda i: (i, 0)), pl.BlockSpec((1, d), lambda i: (0, 0))],
        out_specs=pl.BlockSpec((tm, d), lambda i: (i, 0)),
        compiler_params=_params(("parallel",), 32),
        name="rmsnorm",
    )(x, g.reshape(1, d))


def _add_rmsnorm_body(x_ref, y_ref, g_ref, s_ref, o_ref):
    x = x_ref[...] + y_ref[...]
    s_ref[...] = x
    inv = lax.rsqrt(jnp.mean(x * x, axis=-1, keepdims=True) + EPS)
    o_ref[...] = (x * inv * g_ref[...]).astype(o_ref.dtype)


def _add_rmsnorm(x, y, g):
    m, d = x.shape
    tm = _pick(m, 256, 128)
    row = pl.BlockSpec((tm, d), lambda i: (i, 0))
    return pl.pallas_call(
        _add_rmsnorm_body,
        out_shape=(jax.ShapeDtypeStruct((m, d), F32), jax.ShapeDtypeStruct((m, d), BF16)),
        grid=(m // tm,),
        in_specs=[row, row, pl.BlockSpec((1, d), lambda i: (0, 0))],
        out_specs=(row, row),
        compiler_params=_params(("parallel",), 40),
        name="add_rmsnorm",
    )(x, y, g.reshape(1, d))


def _mm_body(nks, n_extra, epilogue, *refs):
    n_terms = len(nks)
    a_refs = refs[0:2 * n_terms:2]
    w_refs = refs[1:2 * n_terms:2]
    ex_refs = refs[2 * n_terms:2 * n_terms + n_extra]
    o_ref = refs[2 * n_terms + n_extra]
    acc_refs = refs[2 * n_terms + n_extra + 1:]
    k = pl.program_id(2)
    nk = max(nks)

    for t in range(n_terms):
        a_ref, w_ref, acc_ref = a_refs[t], w_refs[t], acc_refs[t]

        @pl.when(k == 0)
        def _(acc_ref=acc_ref):
            acc_ref[...] = jnp.zeros_like(acc_ref)

        def update(a_ref=a_ref, w_ref=w_ref, acc_ref=acc_ref):
            acc_ref[...] += jnp.dot(a_ref[...].astype(BF16), w_ref[...].astype(BF16),
                                    preferred_element_type=F32)

        if nks[t] == nk:
            update()
        else:
            pl.when(k < nks[t])(update)

    @pl.when(k == nk - 1)
    def _():
        o_ref[...] = epilogue([r[...] for r in acc_refs], ex_refs).astype(o_ref.dtype)


def _matmul(terms, n_out, out_dtype, epilogue, extras=(), *, tm=1024, tn=1024, tk=512, name):
    m = terms[0][0].shape[0]
    tm = _pick(m, tm, 512, 256, 128)
    tn = _pick(math.gcd(n_out, *[off for _, _, off in terms]), tn, 512, 256, 128)
    tks = [min(tk, a.shape[1]) for a, _, _ in terms]
    nks = [a.shape[1] // t for (a, _, _), t in zip(terms, tks)]
    nk = max(nks)
    operands, in_specs = [], []
    for (a, w, off), tkt, nkt in zip(terms, tks, nks):
        assert a.shape[1] == w.shape[0] and off % tn == 0 and a.shape[1] % tkt == 0
        ob = off // tn
        operands += [a, w]
        in_specs += [
            pl.BlockSpec((tm, tkt), lambda i, j, k, nkt=nkt: (i, jnp.minimum(k, nkt - 1))),
            pl.BlockSpec((tkt, tn), lambda i, j, k, nkt=nkt, ob=ob: (jnp.minimum(k, nkt - 1), j + ob)),
        ]
    for arr, shape, imap in extras:
        operands.append(arr)
        in_specs.append(pl.BlockSpec(shape, lambda i, j, k, imap=imap: imap(i, j)))
    return pl.pallas_call(
        functools.partial(_mm_body, tuple(nks), len(extras), epilogue),
        out_shape=jax.ShapeDtypeStruct((m, n_out), out_dtype),
        grid=(m // tm, n_out // tn, nk),
        in_specs=in_specs,
        out_specs=pl.BlockSpec((tm, tn), lambda i, j, k: (i, j)),
        scratch_shapes=[pltpu.VMEM((tm, tn), F32) for _ in terms],
        compiler_params=_params(("parallel", "parallel", "arbitrary"), 48),
        name=name,
    )(*operands)


def _epi_identity(accs, ex):
    return accs[0]


def _epi_qk_norm_rope(accs, ex):
    gain_ref, cos_ref, sin_ref = ex
    z = accs[0]
    gain, cos, sin = gain_ref[...], cos_ref[...], sin_ref[...]
    heads = []
    for h in range(z.shape[1] // LANES):
        x = z[:, h * LANES:(h + 1) * LANES]
        inv = lax.rsqrt(jnp.mean(x * x, axis=-1, keepdims=True) + EPS)
        xn = x * inv * gain
        heads.append(xn * cos + pltpu.roll(xn, LANES // 2, 1) * sin)
    return jnp.concatenate(heads, axis=1)


def _epi_gate_mul(accs, ex):
    return jax.nn.sigmoid(accs[0] + ex[0][...]) * accs[1]


def _epi_gate_mul_add(accs, ex):
    return ex[1][...] + jax.nn.sigmoid(accs[0] + ex[0][...]) * accs[1]


def _epi_residual(accs, ex):
    return ex[0][...] + accs[0]


def _epi_ple(accs, ex):
    return ex[0][...] + jax.nn.sigmoid(accs[0]) * accs[1]


def _block_mean_body(k_ref, o_ref, *, nblk):
    x = k_ref[...]
    o_ref[...] = jnp.sum(x.reshape(nblk, MOBA_BLOCK, x.shape[1]), axis=1) * (1.0 / MOBA_BLOCK)


def _block_means(k):
    r, w = k.shape
    n = r // MOBA_BLOCK
    nblk = _pick(n, 8)
    tw = _pick(w, 512, 128)
    return pl.pallas_call(
        functools.partial(_block_mean_body, nblk=nblk),
        out_shape=jax.ShapeDtypeStruct((n, w), F32),
        grid=(n // nblk, w // tw),
        in_specs=[pl.BlockSpec((nblk * MOBA_BLOCK, tw), lambda i, j: (i, j))],
        out_specs=pl.BlockSpec((nblk, tw), lambda i, j: (i, j)),
        compiler_params=_params(("parallel", "parallel"), 32),
        name="moba_block_means",
    )(k)


def _topk_mask_lanes(gate, n_valid, n_sel, n_pick):
    col = lax.broadcasted_iota(I32, gate.shape, 1)
    g = jnp.where(col < n_valid, gate, NEG_INF)
    sel = jnp.zeros(gate.shape, F32)
    for r in range(n_sel):
        mx = jnp.max(g, axis=1, keepdims=True)
        idx = jnp.min(jnp.where(g == mx, col, gate.shape[1]), axis=1, keepdims=True)
        pick = col == idx
        counts = jnp.where(n_pick > r, 1.0, 0.0)
        sel = jnp.where(pick, jnp.maximum(sel, counts), sel)
        g = jnp.where(pick, NEG_INF, g)
    return sel


def _column(x, n):
    col = lax.broadcasted_iota(I32, x.shape, 1)
    return jnp.max(jnp.where(col == n, x, NEG_INF), axis=1, keepdims=True)


def _moba_prompt_body(q_ref, k_ref, v_ref, km_ref, o_ref, m_sc, l_sc, acc_sc, *, nb, n_sel, scale):
    qb = pl.program_id(2)
    blk = MOBA_BLOCK
    q = q_ref[...]
    gate = lax.dot_general(q, km_ref[...].astype(BF16), NT_DIMS, preferred_element_type=F32)
    sel = _topk_mask_lanes(gate, qb, n_sel, qb)

    def scores(n):
        kb = k_ref[pl.ds(pl.multiple_of(n * blk, blk), blk), :].astype(BF16)
        return lax.dot_general(q, kb, NT_DIMS, preferred_element_type=F32) * scale

    def values(n):
        return v_ref[pl.ds(pl.multiple_of(n * blk, blk), blk), :].astype(BF16)

    s = scores(qb)
    row = lax.broadcasted_iota(I32, s.shape, 0)
    colk = lax.broadcasted_iota(I32, s.shape, 1)
    s = jnp.where(colk <= row, s, NEG_INF)
    m = jnp.max(s, axis=1, keepdims=True)
    p = jnp.exp(s - m)
    m_sc[...] = jnp.broadcast_to(m, m_sc.shape)
    l_sc[...] = jnp.broadcast_to(jnp.sum(p, axis=1, keepdims=True), l_sc.shape)
    acc_sc[...] = jnp.dot(p.astype(BF16), values(qb), preferred_element_type=F32)

    def past_block(n, carry):
        picked = _column(sel, n) > 0.0
        s = jnp.where(picked, scores(n), NEG_INF)
        m_prev = m_sc[:, :1]
        m_new = jnp.maximum(m_prev, jnp.max(s, axis=1, keepdims=True))
        alpha = jnp.exp(m_prev - m_new)
        p = jnp.exp(s - m_new)
        l_sc[...] = jnp.broadcast_to(alpha * l_sc[:, :1] + jnp.sum(p, axis=1, keepdims=True), l_sc.shape)
        acc_sc[...] = alpha * acc_sc[...] + jnp.dot(p.astype(BF16), values(n), preferred_element_type=F32)
        m_sc[...] = jnp.broadcast_to(m_new, m_sc.shape)
        return carry

    lax.fori_loop(0, qb, past_block, 0)
    o_ref[...] = (acc_sc[...] / l_sc[:, :1]).astype(o_ref.dtype)


def _moba_prompt(q, k, v, batch, seq, heads):
    hd = q.shape[1] // heads
    assert seq % MOBA_BLOCK == 0 and hd == LANES
    nb = seq // MOBA_BLOCK
    kmean = _block_means(k)
    body = functools.partial(_moba_prompt_body, nb=nb, n_sel=min(MOBA_TOPK, nb - 1), scale=hd ** -0.5)
    qspec = pl.BlockSpec((MOBA_BLOCK, hd), lambda b, h, i: (b * nb + i, h))
    kvspec = pl.BlockSpec((seq, hd), lambda b, h, i: (b, h))
    return pl.pallas_call(
        body,
        out_shape=jax.ShapeDtypeStruct(q.shape, BF16),
        grid=(batch, heads, nb),
        in_specs=[qspec, kvspec, kvspec, pl.BlockSpec((nb, hd), lambda b, h, i: (b, h))],
        out_specs=qspec,
        scratch_shapes=[pltpu.VMEM((MOBA_BLOCK, LANES), F32), pltpu.VMEM((MOBA_BLOCK, LANES), F32),
                        pltpu.VMEM((MOBA_BLOCK, hd), F32)],
        compiler_params=_params(("parallel", "parallel", "arbitrary"), 32),
        name="moba_prompt_attention",
    )(q, k, v, kmean)


def _page_sum_body(pt_ref, k_ref, o_ref, *, ppb):
    first = lax.rem(pl.program_id(1), ppb) == 0
    s = jnp.sum(k_ref[...], axis=0)

    @pl.when(first)
    def _():
        o_ref[...] = s

    @pl.when(jnp.logical_not(first))
    def _():
        o_ref[...] += s


def _moba_sample_body(pt_ref, q_ref, kn_ref, vn_ref, ks_ref, k_ref, v_ref, o_ref,
                      sel_sc, m_sc, l_sc, acc_sc, *, ppb, nbp, n_sel, scale):
    p = pl.program_id(1)
    heads, hd = q_ref.shape
    q = q_ref[...]

    @pl.when(p == 0)
    def _():
        qf = q.astype(F32)
        lane = lax.broadcasted_iota(I32, sel_sc.shape, 1)
        gate = jnp.zeros(sel_sc.shape, F32)
        for n in range(nbp):
            g_n = jnp.sum(qf * (ks_ref[n] * (1.0 / MOBA_BLOCK)), axis=1, keepdims=True)
            gate = jnp.where(lane == n, g_n, gate)
        sel_sc[...] = _topk_mask_lanes(gate, nbp, n_sel, n_sel)
        s_loc = jnp.sum(qf * kn_ref[...], axis=1, keepdims=True) * scale
        m_sc[...] = jnp.broadcast_to(s_loc, m_sc.shape)
        l_sc[...] = jnp.ones_like(l_sc)
        acc_sc[...] = vn_ref[...]

    rows = k_ref.shape[0] * heads
    picked = _column(sel_sc[...], lax.div(p, ppb)) > 0.0
    kb = k_ref[...].reshape(rows, hd).astype(BF16)
    s = lax.dot_general(q, kb, NT_DIMS, preferred_element_type=F32) * scale
    own_head = lax.rem(lax.broadcasted_iota(I32, s.shape, 1), heads) == lax.broadcasted_iota(I32, s.shape, 0)
    s = jnp.where(jnp.logical_and(own_head, picked), s, NEG_INF)
    m_prev = m_sc[:, :1]
    m_new = jnp.maximum(m_prev, jnp.max(s, axis=1, keepdims=True))
    alpha = jnp.exp(m_prev - m_new)
    pr = jnp.exp(s - m_new)
    l_sc[...] = jnp.broadcast_to(alpha * l_sc[:, :1] + jnp.sum(pr, axis=1, keepdims=True), l_sc.shape)
    vb = v_ref[...].reshape(rows, hd).astype(BF16)
    acc_sc[...] = alpha * acc_sc[...] + jnp.dot(pr.astype(BF16), vb, preferred_element_type=F32)
    m_sc[...] = jnp.broadcast_to(m_new, m_sc.shape)

    @pl.when(p == pl.num_programs(1) - 1)
    def _():
        o_ref[...] = (acc_sc[...] / l_sc[:, :1]).astype(o_ref.dtype)


def _moba_sample(q, k_new, v_new, cache_k, cache_v, layer, page_table):
    db, heads, hd = q.shape
    page = cache_k.shape[2]
    n_pages = page_table.shape[1]
    past = n_pages * page
    ppb = MOBA_BLOCK // page
    nbp = past // MOBA_BLOCK
    assert MOBA_BLOCK % page == 0 and past % MOBA_BLOCK == 0 and nbp >= 1 and hd == LANES
    n_sel = min(MOBA_TOPK, nbp)
    pt = page_table.reshape(-1).astype(I32)
    page_spec = pl.BlockSpec((None, None, page, heads, hd),
                             lambda b, p, pt: (layer, pt[b * n_pages + p], 0, 0, 0))
    blk_sums = pl.pallas_call(
        functools.partial(_page_sum_body, ppb=ppb),
        out_shape=jax.ShapeDtypeStruct((db, nbp, heads, hd), F32),
        grid_spec=pltpu.PrefetchScalarGridSpec(
            num_scalar_prefetch=1,
            grid=(db, n_pages),
            in_specs=[page_spec],
            out_specs=pl.BlockSpec((None, None, heads, hd), lambda b, p, pt: (b, lax.div(p, ppb), 0, 0)),
        ),
        compiler_params=_params(("parallel", "arbitrary"), 32),
        name="moba_sample_block_sums",
    )(pt, cache_k)
    tok = pl.BlockSpec((None, heads, hd), lambda b, p, pt: (b, 0, 0))
    return pl.pallas_call(
        functools.partial(_moba_sample_body, ppb=ppb, nbp=nbp, n_sel=n_sel, scale=hd ** -0.5),
        out_shape=jax.ShapeDtypeStruct((db, heads, hd), BF16),
        grid_spec=pltpu.PrefetchScalarGridSpec(
            num_scalar_prefetch=1,
            grid=(db, n_pages),
            in_specs=[tok, tok, tok,
                      pl.BlockSpec((None, nbp, heads, hd), lambda b, p, pt: (b, 0, 0, 0)),
                      page_spec, page_spec],
            out_specs=tok,
            scratch_shapes=[pltpu.VMEM((heads, LANES), F32), pltpu.VMEM((heads, LANES), F32),
                            pltpu.VMEM((heads, LANES), F32), pltpu.VMEM((heads, hd), F32)],
        ),
        compiler_params=_params(("parallel", "arbitrary"), 32),
        name="moba_sample_attention",
    )(pt, q, k_new, v_new, blk_sums, cache_k, cache_v)


def _shift_rows(x, k):
    row = lax.broadcasted_iota(I32, x.shape, 0)
    return jnp.where(row >= k, pltpu.roll(x, k, 0), 0.0)


def _pool_prompt_body(u_ref, o_ref, *, tiles_per_group):
    u = u_ref[...]
    group = lax.div(pl.program_id(1), tiles_per_group)
    acc = u
    span = 1
    for g, w in enumerate(POOL_WINDOWS):
        assert w == 2 * span
        nxt = acc + _shift_rows(acc, span)
        acc = jnp.where(group >= g, nxt, acc)
        span = w
    win = jnp.left_shift(2, group).astype(F32)
    cnt = jnp.minimum(win, (lax.broadcasted_iota(I32, u.shape, 0) + 1).astype(F32))
    o_ref[...] = (acc / cnt - u).astype(o_ref.dtype)


def _pool_prompt(u, batch, seq):
    c = u.shape[1]
    gw = c // len(POOL_WINDOWS)
    assert gw % LANES == 0
    spec = pl.BlockSpec((seq, LANES), lambda b, j: (b, j))
    return pl.pallas_call(
        functools.partial(_pool_prompt_body, tiles_per_group=gw // LANES),
        out_shape=jax.ShapeDtypeStruct(u.shape, BF16),
        grid=(batch, c // LANES),
        in_specs=[spec],
        out_specs=spec,
        compiler_params=_params(("parallel", "parallel"), 32),
        name="pool_prompt",
    )(u)


def _pool_sample_body(st_ref, u_ref, o_ref, *, gw):
    n0 = st_ref.shape[0]
    for g, w in enumerate(POOL_WINDOWS):
        cols = slice(g * gw, (g + 1) * gw)
        u = u_ref[:, cols]
        acc = u
        for r in range(n0 - (w - 1), n0):
            acc = acc + st_ref[r, :, cols]
        o_ref[:, cols] = (acc / float(w) - u).astype(o_ref.dtype)


def _pool_sample(u, state):
    db, c = u.shape
    n0 = state.shape[0]
    assert n0 >= max(POOL_WINDOWS) - 1
    tb = _pick(db, 16)
    return pl.pallas_call(
        functools.partial(_pool_sample_body, gw=c // len(POOL_WINDOWS)),
        out_shape=jax.ShapeDtypeStruct((db, c), BF16),
        grid=(db // tb,),
        in_specs=[pl.BlockSpec((n0, tb, c), lambda i: (0, i, 0)), pl.BlockSpec((tb, c), lambda i: (i, 0))],
        out_specs=pl.BlockSpec((tb, c), lambda i: (i, 0)),
        compiler_params=_params(("parallel",), 32),
        name="pool_sample",
    )(state, u)


def _pool_proj_body(a_ref, w_ref, s_ref, o_ref):
    y = jnp.dot(a_ref[...], w_ref[...].astype(BF16), preferred_element_type=F32)
    o_ref[...] = (y * s_ref[...]).astype(o_ref.dtype)


def _pool_proj(pooled, pool_w, pool_scale):
    m, c = pooled.shape
    groups, gw, _ = pool_w.shape
    tm = _pick(m, 1024, 512, 256, 128)
    return pl.pallas_call(
        _pool_proj_body,
        out_shape=jax.ShapeDtypeStruct((m, c), BF16),
        grid=(groups, m // tm),
        in_specs=[pl.BlockSpec((tm, gw), lambda g, i: (i, g)),
                  pl.BlockSpec((None, gw, gw), lambda g, i: (g, 0, 0)),
                  pl.BlockSpec((1, gw), lambda g, i: (0, g))],
        out_specs=pl.BlockSpec((tm, gw), lambda g, i: (i, g)),
        compiler_params=_params(("parallel", "parallel"), 32),
        name="pool_group_proj",
    )(pooled, pool_w, pool_scale.reshape(1, c))


def _topk_rows(x, k):
    nrows, t = x.shape
    row = lax.broadcasted_iota(I32, x.shape, 0)
    rank = lax.broadcasted_iota(I32, (k, t), 0)
    vals = jnp.zeros((k, t), F32)
    idxs = jnp.zeros((k, t), I32)
    for r in range(k):
        mx = jnp.max(x, axis=0, keepdims=True)
        idx = jnp.min(jnp.where(x == mx, row, nrows), axis=0, keepdims=True)
        vals = jnp.where(rank == r, mx, vals)
        idxs = jnp.where(rank == r, idx, idxs)
        x = jnp.where(row == idx, NEG_INF, x)
    return vals, idxs


def _take_rows(table, idx):
    out = jnp.zeros(idx.shape, table.dtype)
    for a in range(table.shape[0]):
        out = jnp.where(idx == a, table[a:a + 1, :], out)
    return out


def _peer_topk_body(q_ref, sk_ref, e1_ref, e2_ref, g_ref):
    keys = sk_ref.shape[1]
    dk = sk_ref.shape[2]
    k = PEER_TOPK
    s1 = lax.dot_general(sk_ref[0], q_ref[:, :dk], NT_DIMS, preferred_element_type=F32)
    s2 = lax.dot_general(sk_ref[1], q_ref[:, dk:], NT_DIMS, preferred_element_type=F32)
    v1, i1 = _topk_rows(s1, k)
    v2, i2 = _topk_rows(s2, k)
    cand = jnp.concatenate([v1[a:a + 1, :] + v2 for a in range(k)], axis=0)
    top_s, top_c = _topk_rows(cand, k)
    shift = k.bit_length() - 1
    assert k == 1 << shift
    e1_ref[...] = _take_rows(i1, jnp.right_shift(top_c, shift))
    e2_ref[...] = _take_rows(i2, jnp.bitwise_and(top_c, k - 1))
    ex = jnp.exp(top_s - top_s[0:1, :])
    g_ref[...] = ex / jnp.sum(ex, axis=0, keepdims=True)


def _peer_topk(qp, subkeys):
    m = qp.shape[0]
    heads, _, keys, dk = subkeys.shape
    assert dk == LANES
    t = _pick(m, 256, 128)
    out = pl.BlockSpec((None, PEER_TOPK, t), lambda i, h: (h, 0, i))
    shape = (heads, PEER_TOPK, m)
    return pl.pallas_call(
        _peer_topk_body,
        out_shape=(jax.ShapeDtypeStruct(shape, I32), jax.ShapeDtypeStruct(shape, I32),
                   jax.ShapeDtypeStruct(shape, F32)),
        grid=(m // t, heads),
        in_specs=[pl.BlockSpec((t, 2 * dk), lambda i, h: (i, h)),
                  pl.BlockSpec((None, 2, keys, dk), lambda i, h: (h, 0, 0, 0))],
        out_specs=(out, out, out),
        compiler_params=_params(("parallel", "parallel"), 32),
        name="peer_topk",
    )(qp, subkeys)


def _peer_dense_gate_body(e1_ref, e2_ref, g_ref, o_ref, e1_sc, e2_sc, g_sc):
    n_sel = e1_sc.shape[1]
    keys = o_ref.shape[1]
    e1_sc[...] = e1_ref[...].reshape(n_sel, -1).T
    e2_sc[...] = e2_ref[...].reshape(n_sel, -1).T
    g_sc[...] = g_ref[...].reshape(n_sel, -1).T
    key_id = lax.broadcasted_iota(I32, (keys, n_sel), 0)

    def token(n, carry):
        e1 = e1_sc[pl.ds(n, 1), :]
        e2 = e2_sc[pl.ds(n, 1), :]
        g = g_sc[pl.ds(n, 1), :]
        lhs = jnp.where(key_id == e1, g, 0.0).astype(BF16)
        rhs = jnp.where(key_id == e2, 1.0, 0.0).astype(BF16)
        o_ref[n] = lax.dot_general(lhs, rhs, NT_DIMS, preferred_element_type=F32).astype(o_ref.dtype)
        return carry

    lax.fori_loop(0, o_ref.shape[0], token, 0)


def _peer_dense_gate(e1, e2, g, keys):
    heads, k, m = e1.shape
    n_sel = heads * k
    t = _pick(m, 128)
    spec = pl.BlockSpec((heads, k, t), lambda i: (0, 0, i))
    dense = pl.pallas_call(
        _peer_dense_gate_body,
        out_shape=jax.ShapeDtypeStruct((m, keys, keys), BF16),
        grid=(m // t,),
        in_specs=[spec, spec, spec],
        out_specs=pl.BlockSpec((t, keys, keys), lambda i: (i, 0, 0)),
        scratch_shapes=[pltpu.VMEM((t, n_sel), I32), pltpu.VMEM((t, n_sel), I32), pltpu.VMEM((t, n_sel), F32)],
        compiler_params=_params(("parallel",), 32),
        name="peer_dense_gate",
    )(e1, e2, g)
    return dense.reshape(m, keys * keys)


def _peer_experts_body(h_ref, g_ref, u_ref, v_ref, o_ref):
    e = pl.program_id(1)
    a = lax.dot_general(h_ref[...], u_ref[...], NT_DIMS, preferred_element_type=F32)
    gelu = 0.5 * a * (1.0 + lax.erf(a * (2.0 ** -0.5)))
    w = (g_ref[...].astype(F32) * gelu).astype(BF16)
    y = jnp.dot(w, v_ref[...], preferred_element_type=F32)

    @pl.when(e == 0)
    def _():
        o_ref[...] = y

    @pl.when(e != 0)
    def _():
        o_ref[...] += y


def _peer_experts(hn, dense_gate, u_tab, v_tab):
    m, d = hn.shape
    n_exp = u_tab.shape[0]
    tm = _pick(m, 512, 256, 128)
    te = _pick(n_exp, 512, 256, 128)
    tab = pl.BlockSpec((te, d), lambda i, e: (e, 0))
    return pl.pallas_call(
        _peer_experts_body,
        out_shape=jax.ShapeDtypeStruct((m, d), F32),
        grid=(m // tm, n_exp // te),
        in_specs=[pl.BlockSpec((tm, d), lambda i, e: (i, 0)),
                  pl.BlockSpec((tm, te), lambda i, e: (i, e)), tab, tab],
        out_specs=pl.BlockSpec((tm, d), lambda i, e: (i, 0)),
        compiler_params=_params(("parallel", "arbitrary"), 56),
        name="peer_experts",
    )(hn, dense_gate, u_tab, v_tab)


def _peer_ffn(hn, wq, subkeys, u_tab, v_tab):
    qp = _matmul([(hn, wq, 0)], wq.shape[1], BF16, _epi_identity, name="peer_query_proj")
    e1, e2, g = _peer_topk(qp, subkeys)
    return _peer_experts(hn, _peer_dense_gate(e1, e2, g, subkeys.shape[2]), u_tab, v_tab)


def _rope_tables(pos):
    half = LANES // 2
    freqs = ROPE_THETA ** (-jnp.arange(half, dtype=F32) / half)
    ang = pos.astype(F32)[:, None] * freqs[None, :]
    cos, sin = jnp.cos(ang), jnp.sin(ang)
    return jnp.concatenate([cos, cos], axis=1), jnp.concatenate([-sin, sin], axis=1)


def _trunk_layer(x, p_emb, pos, attend, pool, w):
    m, d = x.shape
    attn_w = w["w_up_attn"].shape[0]
    pool_w = w["w_up_pool"].shape[0]
    if pos.shape[0] % LANES != 0:
        pos = jnp.tile(pos, m // pos.shape[0])
    n_pos = pos.shape[0]
    h = _rmsnorm(x, w["norm_mix"])
    cos, sin = _rope_tables(pos)

    tm = _pick(n_pos, 1024, 512, 256, 128)
    assert m % tm == 0
    pos_blocks = n_pos // tm

    def rope_extras(gain):
        return [(gain.reshape(1, LANES), (1, LANES), lambda i, j: (0, 0)),
                (cos, (tm, LANES), lambda i, j: (i % pos_blocks, 0)),
                (sin, (tm, LANES), lambda i, j: (i % pos_blocks, 0))]

    q = _matmul([(h, w["w_in"], 0)], attn_w, BF16, _epi_qk_norm_rope, rope_extras(w["q_norm"]),
                tm=tm, name="in_proj_q")
    k = _matmul([(h, w["w_in"], attn_w)], attn_w, F32, _epi_qk_norm_rope, rope_extras(w["k_norm"]),
                tm=tm, name="in_proj_k")
    v = _matmul([(h, w["w_in"], 2 * attn_w)], attn_w, F32, _epi_identity, name="in_proj_v")
    u = _matmul([(h, w["w_in"], 3 * attn_w)], pool_w, F32, _epi_identity, name="in_proj_u")

    y_attn = attend(q, k, v)
    y_pool = _pool_proj(pool(u), w["pool_w"], w["pool_scale"])

    bias = w["b_gate"].reshape(1, 2 * d)
    tn = _pick(d, 1024, 512, 256, 128)
    gated = _matmul([(h, w["w_gate"], 0), (y_attn, w["w_up_attn"], 0)], d, F32, _epi_gate_mul,
                    [(bias, (1, tn), lambda i, j: (0, j))], tn=tn, name="gate_attn")
    tmm = _pick(m, 1024, 512, 256, 128)
    merged = _matmul([(h, w["w_gate"], d), (y_pool, w["w_up_pool"], 0)], d, BF16, _epi_gate_mul_add,
                     [(bias, (1, tn), lambda i, j: (0, j + d // tn)),
                      (gated, (tmm, tn), lambda i, j: (i, j))], tn=tn, name="gate_pool_merge")
    x = _matmul([(merged, w["w_out"], 0)], d, F32, _epi_residual,
                [(x, (tmm, tn), lambda i, j: (i, j))], tn=tn, name="out_proj")

    ffn = _peer_ffn(_rmsnorm(x, w["norm_ffn"]), w["peer_wq"], w["peer_subkeys"], w["peer_u"], w["peer_v"])
    x, xn = _add_rmsnorm(x, ffn, w["ple_norm"])
    x = _matmul([(xn, w["ple_gate"], 0), (p_emb, w["ple_proj"], 0)], d, F32, _epi_ple,
                [(x, (tmm, tn), lambda i, j: (i, j))], tn=tn, name="ple")
    return x, k, v, u


_MATMUL_WEIGHTS = ("w_in", "w_up_attn", "w_up_pool", "w_gate", "w_out", "peer_wq", "peer_subkeys",
                   "peer_u", "peer_v", "ple_proj", "ple_gate")


def kernel(x_prompt, x_sample, cache_k, cache_v, state_pool, page_table, p_prompt, p_sample, norm_mix, w_in,
           q_norm, k_norm, pool_w, pool_scale, w_up_attn, w_up_pool, w_gate, b_gate, w_out, norm_ffn, peer_wq,
           peer_subkeys, peer_u, peer_v, ple_proj, ple_norm, ple_gate):
    batch, seq, d = x_prompt.shape
    db, dec_seq, _ = x_sample.shape
    depth = norm_mix.shape[0]
    heads, hd = cache_k.shape[3], cache_k.shape[4]
    past = page_table.shape[1] * cache_k.shape[2]
    assert dec_seq == 1, "one new token per sample sequence"
    weights = dict(norm_mix=norm_mix, w_in=w_in, q_norm=q_norm, k_norm=k_norm, pool_w=pool_w,
                   pool_scale=pool_scale, w_up_attn=w_up_attn, w_up_pool=w_up_pool, w_gate=w_gate, b_gate=b_gate,
                   w_out=w_out, norm_ffn=norm_ffn, peer_wq=peer_wq, peer_subkeys=peer_subkeys, peer_u=peer_u,
                   peer_v=peer_v, ple_proj=ple_proj, ple_norm=ple_norm, ple_gate=ple_gate)
    pos_p = jnp.arange(seq, dtype=I32)
    pos_s = past + jnp.arange(dec_seq, dtype=I32)
    yp = x_prompt.reshape(batch * seq, d)
    ys = x_sample.reshape(db * dec_seq, d)
    outs = [[] for _ in range(6)]
    for l in range(depth):
        w = {name: (t[l].astype(BF16) if name in _MATMUL_WEIGHTS else t[l]) for name, t in weights.items()}

        def attend_p(q, k, v):
            return _moba_prompt(q, k, v, batch, seq, heads)

        def attend_s(q, k, v):
            y = _moba_sample(q.reshape(db, heads, hd), k.reshape(db, heads, hd), v.reshape(db, heads, hd),
                             cache_k, cache_v, l, page_table)
            return y.reshape(db, heads * hd)

        yp, kp, vp, up = _trunk_layer(yp, p_prompt[l].reshape(batch * seq, -1), pos_p, attend_p,
                                      lambda u: _pool_prompt(u, batch, seq), w)
        ys, ks, vs, us = _trunk_layer(ys, p_sample[l].reshape(db * dec_seq, -1), pos_s, attend_s,
                                      lambda u, l=l: _pool_sample(u, jnp.swapaxes(state_pool[l], 0, 1)), w)
        n_state = state_pool.shape[2]
        pool_p = up.reshape(batch, seq, -1)[:, seq - n_state:]
        pool_s = jnp.concatenate([state_pool[l], us.reshape(db, dec_seq, -1)], axis=1)[:, dec_seq:]
        for dst, val in zip(outs, (kp.reshape(batch, seq, heads, hd), vp.reshape(batch, seq, heads, hd), pool_p,
                                   ks.reshape(db, dec_seq, heads, hd), vs.reshape(db, dec_seq, heads, hd),
                                   pool_s)):
            dst.append(val)
    return (yp.reshape(batch, seq, d), ys.reshape(db, dec_seq, d)) + tuple(jnp.stack(o) for o in outs)
```

```python
import functools
import math

import jax
import jax.numpy as jnp
from jax import lax
from jax.experimental import pallas as pl
from jax.experimental.pallas import tpu as pltpu

F32 = jnp.float32
BF16 = jnp.bfloat16
I32 = jnp.int32

EPS = 1e-6
MOBA_BLOCK = 256
MOBA_TOPK = 3
ROPE_THETA = 10000.0
POOL_WINDOWS = (2, 4, 8, 16)
PEER_TOPK = 16

LANES = 128
MIB = 2 ** 20
NT_DIMS = (((1,), (1,)), ((), ()))
NEG_INF = float("-inf")


def _params(semantics, vmem_mib):
    return pltpu.CompilerParams(dimension_semantics=semantics, vmem_limit_bytes=vmem_mib * MIB)


def _pick(n, *candidates):
    for c in candidates:
        if n % c == 0:
            return c
    return n


def _rmsnorm_body(x_ref, g_ref, o_ref):
    x = x_ref[...]
    inv = lax.rsqrt(jnp.mean(x * x, axis=-1, keepdims=True) + EPS)
    o_ref[...] = (x * inv * g_ref[...]).astype(o_ref.dtype)


def _rmsnorm(x, g):
    m, d = x.shape
    tm = _pick(m, 256, 128)
    return pl.pallas_call(
        _rmsnorm_body,
        out_shape=jax.ShapeDtypeStruct((m, d), BF16),
        grid=(m // tm,),
        in_specs=[pl.BlockSpec((tm, d), lambda i: (i, 0)), pl.BlockSpec((1, d), lambda i: (0, 0))],
        out_specs=pl.BlockSpec((tm, d), lambda i: (i, 0)),
        compiler_params=_params(("parallel",), 32),
        name="rmsnorm",
    )(x, g.reshape(1, d))


def _add_rmsnorm_body(x_ref, y_ref, g_ref, s_ref, o_ref):
    x = x_ref[...] + y_ref[...]
    s_ref[...] = x
    inv = lax.rsqrt(jnp.mean(x * x, axis=-1, keepdims=True) + EPS)
    o_ref[...] = (x * inv * g_ref[...]).astype(o_ref.dtype)


def _add_rmsnorm(x, y, g):
    m, d = x.shape
    tm = _pick(m, 256, 128)
    row = pl.BlockSpec((tm, d), lambda i: (i, 0))
    return pl.pallas_call(
        _add_rmsnorm_body,
        out_shape=(jax.ShapeDtypeStruct((m, d), F32), jax.ShapeDtypeStruct((m, d), BF16)),
        grid=(m // tm,),
        in_specs=[row, row, pl.BlockSpec((1, d), lambda i: (0, 0))],
        out_specs=(row, row),
        compiler_params=_params(("parallel",), 40),
        name="add_rmsnorm",
    )(x, y, g.reshape(1, d))


VMEM_BUDGET_MIB = 48


def _mm_body(n_terms, n_extra, epilogue, *refs):
    a_refs = refs[0:2 * n_terms:2]
    w_refs = refs[1:2 * n_terms:2]
    ex_refs = refs[2 * n_terms:2 * n_terms + n_extra]
    o_ref = refs[2 * n_terms + n_extra]
    accs = [jnp.dot(a_ref[...].astype(BF16), w_ref[...].astype(BF16), preferred_element_type=F32)
            for a_ref, w_ref in zip(a_refs, w_refs)]
    o_ref[...] = epilogue(accs, ex_refs).astype(o_ref.dtype)


def _matmul_tiles(terms, n_out, out_dtype, extra_bytes_per_elem):
    m = terms[0][0].shape[0]
    gcd_n = math.gcd(n_out, *[off for _, _, off in terms])
    candidates = ((1024, 1024), (1024, 512), (512, 512), (512, 256), (256, 256), (128, 256), (128, 128),
                  (m, 256), (m, 128))
    for tm, tn in candidates:
        if m % tm or gcd_n % tn:
            continue
        blocks = sum(tm * a.shape[1] * a.dtype.itemsize + a.shape[1] * tn * w.dtype.itemsize for a, w, _ in terms)
        blocks += tm * tn * (jnp.dtype(out_dtype).itemsize + extra_bytes_per_elem)
        if 2 * blocks <= VMEM_BUDGET_MIB * MIB:
            return tm, tn
    raise ValueError("no matmul tiling fits VMEM")


def _matmul(terms, n_out, out_dtype, epilogue, extras=(), *, tiles, name):
    m = terms[0][0].shape[0]
    tm, tn = tiles
    operands, in_specs = [], []
    for a, w, off in terms:
        assert a.shape[1] == w.shape[0] and off % tn == 0
        kdim = a.shape[1]
        operands += [a, w]
        in_specs += [pl.BlockSpec((tm, kdim), lambda i, j: (i, 0)),
                     pl.BlockSpec((kdim, tn), lambda i, j, ob=off // tn: (0, j + ob))]
    for arr, shape, imap in extras:
        operands.append(arr)
        in_specs.append(pl.BlockSpec(shape, imap))
    return pl.pallas_call(
        functools.partial(_mm_body, len(terms), len(extras), epilogue),
        out_shape=jax.ShapeDtypeStruct((m, n_out), out_dtype),
        grid=(m // tm, n_out // tn),
        in_specs=in_specs,
        out_specs=pl.BlockSpec((tm, tn), lambda i, j: (i, j)),
        compiler_params=_params(("parallel", "parallel"), VMEM_BUDGET_MIB + 8),
        name=name,
    )(*operands)


def _epi_identity(accs, ex):
    return accs[0]


def _epi_qk_norm_rope(accs, ex):
    gain_ref, cos_ref, sin_ref = ex
    z = accs[0]
    gain, cos, sin = gain_ref[...], cos_ref[...], sin_ref[...]
    heads = []
    for h in range(z.shape[1] // LANES):
        x = z[:, h * LANES:(h + 1) * LANES]
        inv = lax.rsqrt(jnp.mean(x * x, axis=-1, keepdims=True) + EPS)
        xn = x * inv * gain
        heads.append(xn * cos + pltpu.roll(xn, LANES // 2, 1) * sin)
    return jnp.concatenate(heads, axis=1)


def _epi_gate_mul(accs, ex):
    return jax.nn.sigmoid(accs[0] + ex[0][...]) * accs[1]


def _epi_gate_mul_add(accs, ex):
    return ex[1][...] + jax.nn.sigmoid(accs[0] + ex[0][...]) * accs[1]


def _epi_residual(accs, ex):
    return ex[0][...] + accs[0]


def _epi_ple(accs, ex):
    return ex[0][...] + jax.nn.sigmoid(accs[0]) * accs[1]


def _block_mean_body(k_ref, o_ref, *, nblk):
    x = k_ref[...]
    o_ref[...] = jnp.sum(x.reshape(nblk, MOBA_BLOCK, x.shape[1]), axis=1) * (1.0 / MOBA_BLOCK)


def _block_means(k):
    r, w = k.shape
    n = r // MOBA_BLOCK
    nblk = _pick(n, 8)
    tw = _pick(w, 512, 128)
    return pl.pallas_call(
        functools.partial(_block_mean_body, nblk=nblk),
        out_shape=jax.ShapeDtypeStruct((n, w), F32),
        grid=(n // nblk, w // tw),
        in_specs=[pl.BlockSpec((nblk * MOBA_BLOCK, tw), lambda i, j: (i, j))],
        out_specs=pl.BlockSpec((nblk, tw), lambda i, j: (i, j)),
        compiler_params=_params(("parallel", "parallel"), 32),
        name="moba_block_means",
    )(k)


def _topk_mask(gate, axis, n_valid, n_sel, n_pick):
    pos = lax.broadcasted_iota(I32, gate.shape, axis)
    g = jnp.where(pos < n_valid, gate, NEG_INF)
    sel = jnp.zeros(gate.shape, F32)
    for r in range(n_sel):
        mx = jnp.max(g, axis=axis, keepdims=True)
        idx = jnp.min(jnp.where(g == mx, pos, gate.shape[axis]), axis=axis, keepdims=True)
        pick = pos == idx
        counts = jnp.where(n_pick > r, 1.0, 0.0)
        sel = jnp.where(pick, jnp.maximum(sel, counts), sel)
        g = jnp.where(pick, NEG_INF, g)
    return sel


def _column(x, n):
    col = lax.broadcasted_iota(I32, x.shape, 1)
    return jnp.max(jnp.where(col == n, x, NEG_INF), axis=1, keepdims=True)


def _moba_prompt_body(q_ref, k_ref, v_ref, km_ref, o_ref, k16_sc, vt16_sc, vtg16_sc, sel_sc, m_sc, l_sc, acc_sc,
                      *, nb, group, n_sel, scale):
    qb = pl.program_id(2)
    blk = MOBA_BLOCK

    @pl.when(qb == 0)
    def _():
        for n in range(nb):
            rows = slice(n * blk, (n + 1) * blk)
            k16_sc[n] = k_ref[rows, :].astype(BF16)
            v_t = v_ref[rows, :].T.astype(BF16)
            vt16_sc[n] = v_t
            vtg16_sc[n // group, :, (n % group) * blk:(n % group + 1) * blk] = v_t

    q = q_ref[...]
    gate_t = lax.dot_general(km_ref[...].astype(BF16), q, NT_DIMS, preferred_element_type=F32)
    sel_sc[...] = _topk_mask(gate_t, 0, qb, n_sel, qb)

    s = lax.dot_general(k16_sc[qb], q, NT_DIMS, preferred_element_type=F32) * scale
    key = lax.broadcasted_iota(I32, s.shape, 0)
    qry = lax.broadcasted_iota(I32, s.shape, 1)
    s = jnp.where(key <= qry, s, NEG_INF)
    m = jnp.max(s, axis=0, keepdims=True)
    p = jnp.exp(s - m)
    m_sc[...] = m
    l_sc[...] = jnp.sum(p, axis=0, keepdims=True)
    acc_sc[...] = jnp.dot(vt16_sc[qb], p.astype(BF16), preferred_element_type=F32)

    def past_blocks(g, carry):
        first = pl.multiple_of(g * group, group)
        kb = k16_sc[pl.ds(first, group)].reshape(group * blk, kb_cols)
        s = lax.dot_general(kb, q, NT_DIMS, preferred_element_type=F32) * scale
        mask = jnp.concatenate([jnp.broadcast_to(sel_sc[pl.ds(first + j, 1), :], (blk, blk))
                                for j in range(group)], axis=0)
        s = jnp.where(mask > 0.0, s, NEG_INF)
        m_prev = m_sc[...]
        m_new = jnp.maximum(m_prev, jnp.max(s, axis=0, keepdims=True))
        alpha = jnp.exp(m_prev - m_new)
        p = jnp.exp(s - m_new)
        l_sc[...] = alpha * l_sc[...] + jnp.sum(p, axis=0, keepdims=True)
        acc_sc[...] = alpha * acc_sc[...] + jnp.dot(vtg16_sc[g], p.astype(BF16), preferred_element_type=F32)
        m_sc[...] = m_new
        return carry

    kb_cols = k16_sc.shape[2]
    lax.fori_loop(0, lax.div(qb + group - 1, group), past_blocks, 0)
    o_ref[...] = (acc_sc[...] / l_sc[...]).T.astype(o_ref.dtype)


def _moba_prompt(q, k, v, batch, seq, heads):
    hd = q.shape[1] // heads
    assert seq % MOBA_BLOCK == 0 and hd == LANES
    nb = seq // MOBA_BLOCK
    group = _pick(nb, 4, 2, 1)
    kmean = _block_means(k)
    body = functools.partial(_moba_prompt_body, nb=nb, group=group, n_sel=min(MOBA_TOPK, nb - 1), scale=hd ** -0.5)
    qspec = pl.BlockSpec((MOBA_BLOCK, hd), lambda b, h, i: (b * nb + i, h))
    kvspec = pl.BlockSpec((seq, hd), lambda b, h, i: (b, h))
    return pl.pallas_call(
        body,
        out_shape=jax.ShapeDtypeStruct(q.shape, BF16),
        grid=(batch, heads, nb),
        in_specs=[qspec, kvspec, kvspec, pl.BlockSpec((nb, hd), lambda b, h, i: (b, h))],
        out_specs=qspec,
        scratch_shapes=[pltpu.VMEM((nb, MOBA_BLOCK, hd), BF16), pltpu.VMEM((nb, hd, MOBA_BLOCK), BF16),
                        pltpu.VMEM((nb // group, hd, group * MOBA_BLOCK), BF16),
                        pltpu.VMEM((nb, MOBA_BLOCK), F32), pltpu.VMEM((1, MOBA_BLOCK), F32),
                        pltpu.VMEM((1, MOBA_BLOCK), F32), pltpu.VMEM((hd, MOBA_BLOCK), F32)],
        compiler_params=_params(("parallel", "parallel", "arbitrary"), 32),
        name="moba_prompt_attention",
    )(q, k, v, kmean)


SAMPLE_PAGES_PER_STEP = 4


def _moba_sample_body(pt_ref, q_ref, kn_ref, vn_ref, *refs, ppb, nbp, n_pages, group, n_sel, scale):
    k_refs, v_refs, o_ref = refs[:group], refs[group:2 * group], refs[2 * group]
    kall_sc, ksum_sc, sel_sc, m_sc, l_sc, acc_sc = refs[2 * group + 1:]
    step = pl.program_id(1)
    k_steps = n_pages // group
    blocks_per_step = group // ppb
    heads, hd = q_ref.shape
    rows = k_refs[0].shape[0] * heads
    q = q_ref[...]

    @pl.when(step < k_steps)
    def _():
        for bi in range(blocks_per_step):
            total = None
            for g in range(bi * ppb, (bi + 1) * ppb):
                page = k_refs[g][...]
                kall_sc[step * group + g] = page.reshape(rows, hd).astype(BF16)
                part = jnp.sum(page, axis=0)
                total = part if total is None else total + part
            ksum_sc[step * blocks_per_step + bi] = total

    @pl.when(step == k_steps)
    def _():
        qf = q.astype(F32)
        lane = lax.broadcasted_iota(I32, sel_sc.shape, 1)
        gate = jnp.zeros(sel_sc.shape, F32)
        for n in range(nbp):
            g_n = jnp.sum(qf * (ksum_sc[n] * (1.0 / MOBA_BLOCK)), axis=1, keepdims=True)
            gate = jnp.where(lane == n, g_n, gate)
        sel_sc[...] = _topk_mask(gate, 1, nbp, n_sel, n_sel)
        s_loc = jnp.sum(qf * kn_ref[...], axis=1, keepdims=True) * scale
        m_sc[...] = jnp.broadcast_to(s_loc, m_sc.shape)
        l_sc[...] = jnp.ones_like(l_sc)
        acc_sc[...] = vn_ref[...]

    @pl.when(step >= k_steps)
    def _():
        vstep = step - k_steps
        sel = sel_sc[...]
        scores = []
        for g in range(group):
            picked = _column(sel, vstep * blocks_per_step + g // ppb) > 0.0
            s = lax.dot_general(q, kall_sc[vstep * group + g], NT_DIMS, preferred_element_type=F32) * scale
            own_head = (lax.rem(lax.broadcasted_iota(I32, s.shape, 1), heads)
                        == lax.broadcasted_iota(I32, s.shape, 0))
            scores.append(jnp.where(jnp.logical_and(own_head, picked), s, NEG_INF))
        m_prev = m_sc[:, :1]
        m_new = m_prev
        for s in scores:
            m_new = jnp.maximum(m_new, jnp.max(s, axis=1, keepdims=True))
        alpha = jnp.exp(m_prev - m_new)
        l_new = alpha * l_sc[:, :1]
        acc = alpha * acc_sc[...]
        for g, s in enumerate(scores):
            pr = jnp.exp(s - m_new)
            l_new = l_new + jnp.sum(pr, axis=1, keepdims=True)
            vb = v_refs[g][...].reshape(rows, hd).astype(BF16)
            acc = acc + jnp.dot(pr.astype(BF16), vb, preferred_element_type=F32)
        m_sc[...] = jnp.broadcast_to(m_new, m_sc.shape)
        l_sc[...] = jnp.broadcast_to(l_new, l_sc.shape)
        acc_sc[...] = acc

    @pl.when(step == pl.num_programs(1) - 1)
    def _():
        o_ref[...] = (acc_sc[...] / l_sc[:, :1]).astype(o_ref.dtype)


def _moba_sample(q, k_new, v_new, cache_k, cache_v, layer, page_table):
    db, heads, hd = q.shape
    page = cache_k.shape[2]
    n_pages = page_table.shape[1]
    past = n_pages * page
    ppb = MOBA_BLOCK // page
    nbp = past // MOBA_BLOCK
    assert MOBA_BLOCK % page == 0 and past % MOBA_BLOCK == 0 and nbp >= 1 and hd == LANES
    group = SAMPLE_PAGES_PER_STEP if n_pages % SAMPLE_PAGES_PER_STEP == 0 else ppb
    assert group % ppb == 0 and n_pages % group == 0
    k_steps = n_pages // group
    pt = page_table.reshape(-1).astype(I32)

    def k_page(g):
        return pl.BlockSpec((None, None, page, heads, hd), lambda b, s, pt: (
            layer, pt[b * n_pages + jnp.minimum(s, k_steps - 1) * group + g], 0, 0, 0))

    def v_page(g):
        return pl.BlockSpec((None, None, page, heads, hd), lambda b, s, pt: (
            layer, pt[b * n_pages + jnp.maximum(s - k_steps, 0) * group + g], 0, 0, 0))

    tok = pl.BlockSpec((None, heads, hd), lambda b, s, pt: (b, 0, 0))
    body = functools.partial(_moba_sample_body, ppb=ppb, nbp=nbp, n_pages=n_pages, group=group,
                             n_sel=min(MOBA_TOPK, nbp), scale=hd ** -0.5)
    return pl.pallas_call(
        body,
        out_shape=jax.ShapeDtypeStruct((db, heads, hd), BF16),
        grid_spec=pltpu.PrefetchScalarGridSpec(
            num_scalar_prefetch=1,
            grid=(db, 2 * k_steps),
            in_specs=[tok, tok, tok] + [k_page(g) for g in range(group)] + [v_page(g) for g in range(group)],
            out_specs=tok,
            scratch_shapes=[pltpu.VMEM((n_pages, page * heads, hd), BF16), pltpu.VMEM((nbp, heads, hd), F32),
                            pltpu.VMEM((heads, LANES), F32), pltpu.VMEM((heads, LANES), F32),
                            pltpu.VMEM((heads, LANES), F32), pltpu.VMEM((heads, hd), F32)],
        ),
        compiler_params=_params(("parallel", "arbitrary"), 48),
        name="moba_sample_attention",
    )(pt, q, k_new, v_new, *([cache_k] * group), *([cache_v] * group))


def _shift_rows(x, k):
    row = lax.broadcasted_iota(I32, x.shape, 0)
    return jnp.where(row >= k, pltpu.roll(x, k, 0), 0.0)


def _pool_prompt_body(u_ref, o_ref, *, tiles_per_group):
    u = u_ref[...]
    group = lax.div(pl.program_id(1), tiles_per_group)
    acc = u
    span = 1
    for g, w in enumerate(POOL_WINDOWS):
        assert w == 2 * span
        nxt = acc + _shift_rows(acc, span)
        acc = jnp.where(group >= g, nxt, acc)
        span = w
    win = jnp.left_shift(2, group).astype(F32)
    cnt = jnp.minimum(win, (lax.broadcasted_iota(I32, u.shape, 0) + 1).astype(F32))
    o_ref[...] = (acc / cnt - u).astype(o_ref.dtype)


def _pool_prompt(u, batch, seq):
    c = u.shape[1]
    gw = c // len(POOL_WINDOWS)
    assert gw % LANES == 0
    spec = pl.BlockSpec((seq, LANES), lambda b, j: (b, j))
    return pl.pallas_call(
        functools.partial(_pool_prompt_body, tiles_per_group=gw // LANES),
        out_shape=jax.ShapeDtypeStruct(u.shape, BF16),
        grid=(batch, c // LANES),
        in_specs=[spec],
        out_specs=spec,
        compiler_params=_params(("parallel", "parallel"), 32),
        name="pool_prompt",
    )(u)


def _pool_sample_body(st_ref, u_ref, o_ref, *, gw):
    n0 = st_ref.shape[0]
    for g, w in enumerate(POOL_WINDOWS):
        cols = slice(g * gw, (g + 1) * gw)
        u = u_ref[:, cols]
        acc = u
        for r in range(n0 - (w - 1), n0):
            acc = acc + st_ref[r, :, cols]
        o_ref[:, cols] = (acc / float(w) - u).astype(o_ref.dtype)


def _pool_sample(u, state):
    db, c = u.shape
    n0 = state.shape[0]
    assert n0 >= max(POOL_WINDOWS) - 1
    tb = _pick(db, 16)
    return pl.pallas_call(
        functools.partial(_pool_sample_body, gw=c // len(POOL_WINDOWS)),
        out_shape=jax.ShapeDtypeStruct((db, c), BF16),
        grid=(db // tb,),
        in_specs=[pl.BlockSpec((n0, tb, c), lambda i: (0, i, 0)), pl.BlockSpec((tb, c), lambda i: (i, 0))],
        out_specs=pl.BlockSpec((tb, c), lambda i: (i, 0)),
        compiler_params=_params(("parallel",), 32),
        name="pool_sample",
    )(state, u)


def _pool_proj_body(a_ref, w_ref, s_ref, o_ref):
    y = jnp.dot(a_ref[...], w_ref[...].astype(BF16), preferred_element_type=F32)
    o_ref[...] = (y * s_ref[...]).astype(o_ref.dtype)


def _pool_proj(pooled, pool_w, pool_scale):
    m, c = pooled.shape
    groups, gw, _ = pool_w.shape
    tm = _pick(m, 1024, 512, 256, 128)
    return pl.pallas_call(
        _pool_proj_body,
        out_shape=jax.ShapeDtypeStruct((m, c), BF16),
        grid=(groups, m // tm),
        in_specs=[pl.BlockSpec((tm, gw), lambda g, i: (i, g)),
                  pl.BlockSpec((None, gw, gw), lambda g, i: (g, 0, 0)),
                  pl.BlockSpec((1, gw), lambda g, i: (0, g))],
        out_specs=pl.BlockSpec((tm, gw), lambda g, i: (i, g)),
        compiler_params=_params(("parallel", "parallel"), 32),
        name="pool_group_proj",
    )(pooled, pool_w, pool_scale.reshape(1, c))


def _topk_rows(x, k):
    nrows, t = x.shape
    row = lax.broadcasted_iota(I32, x.shape, 0)
    rank = lax.broadcasted_iota(I32, (k, t), 0)
    vals = jnp.zeros((k, t), F32)
    for r in range(k):
        mx = jnp.max(x, axis=0, keepdims=True)
        idx = jnp.min(jnp.where(x == mx, row, nrows), axis=0, keepdims=True)
        vals = jnp.where(rank == r, mx, vals)
        x = jnp.where(row == idx, NEG_INF, x)
    return vals


PEER_STAT_ROWS = 8


def _peer_topk_body(q_ref, sk_ref, st_ref, stat_ref):
    keys = sk_ref.shape[1]
    dk = sk_ref.shape[2]
    k = PEER_TOPK
    s1 = lax.dot_general(sk_ref[0], q_ref[:, :dk], NT_DIMS, preferred_element_type=F32)
    s2 = lax.dot_general(sk_ref[1], q_ref[:, dk:], NT_DIMS, preferred_element_type=F32)
    st_ref[:keys, :] = s1
    st_ref[keys:, :] = s2
    v1 = _topk_rows(s1, k)
    v2 = _topk_rows(s2, k)
    cand = jnp.concatenate([v1[a:a + 1, :] + v2 for a in range(k)], axis=0)
    top_s = _topk_rows(cand, k)
    mx = top_s[0:1, :]
    z = jnp.sum(jnp.exp(top_s - mx), axis=0, keepdims=True)
    row = lax.broadcasted_iota(I32, stat_ref.shape, 0)
    stat_ref[...] = jnp.where(row == 0, top_s[k - 1:k, :], jnp.where(row == 1, mx + jnp.log(z), 0.0))


def _peer_topk(qp, subkeys):
    m = qp.shape[0]
    heads, _, keys, dk = subkeys.shape
    assert dk == LANES
    t = _pick(m, 256, 128)
    return pl.pallas_call(
        _peer_topk_body,
        out_shape=(jax.ShapeDtypeStruct((heads * 2 * keys, m), F32),
                   jax.ShapeDtypeStruct((heads, PEER_STAT_ROWS, m), F32)),
        grid=(m // t, heads),
        in_specs=[pl.BlockSpec((t, 2 * dk), lambda i, h: (i, h)),
                  pl.BlockSpec((None, 2, keys, dk), lambda i, h: (h, 0, 0, 0))],
        out_specs=(pl.BlockSpec((2 * keys, t), lambda i, h: (h, i)),
                   pl.BlockSpec((None, PEER_STAT_ROWS, t), lambda i, h: (h, 0, i))),
        compiler_params=_params(("parallel", "parallel"), 32),
        name="peer_topk",
    )(qp, subkeys)


PEER_OUT_CHUNK = 1024


def _peer_experts_body(h_ref, st_ref, stat_ref, u_ref, v_ref, o_ref, *, keys, heads):
    e = pl.program_id(1)
    te, d = u_ref.shape
    tm = h_ref.shape[0]
    per_tile = te // keys

    @pl.when(e == 0)
    def _():
        o_ref[...] = jnp.zeros_like(o_ref)

    a_t = lax.dot_general(u_ref[...], h_ref[...], NT_DIMS, preferred_element_type=F32)
    parts = []
    for ai in range(per_tile):
        a = e * per_tile + ai
        g_t = jnp.zeros((keys, tm), F32)
        for h in range(heads):
            tau, shift = stat_ref[h, 0:1, :], stat_ref[h, 1:2, :]
            s1 = st_ref[pl.ds(2 * h * keys + a, 1), :]
            t = s1 + st_ref[(2 * h + 1) * keys:(2 * h + 2) * keys, :]
            g_t = g_t + jnp.where(t >= tau, jnp.exp(t - shift), 0.0)
        parts.append(g_t)
    g_t = jnp.concatenate(parts, axis=0) if per_tile > 1 else parts[0]
    gelu = 0.5 * a_t * (1.0 + lax.erf(a_t * (2.0 ** -0.5)))
    w = (g_t * gelu).T.astype(BF16)
    for c in range(0, d, PEER_OUT_CHUNK):
        cols = slice(c, min(c + PEER_OUT_CHUNK, d))
        o_ref[:, cols] += jnp.dot(w, v_ref[:, cols], preferred_element_type=F32)


def _peer_experts(hn, scores_t, stats, u_tab, v_tab):
    m, d = hn.shape
    n_exp = u_tab.shape[0]
    heads = stats.shape[0]
    keys = scores_t.shape[0] // (2 * heads)
    assert n_exp == keys * keys
    tm = _pick(m, 512, 256, 128)
    te = _pick(n_exp, 512, 256, 128)
    assert te % keys == 0
    once = pl.Buffered(1)
    tab = pl.BlockSpec((te, d), lambda i, e: (e, 0))
    return pl.pallas_call(
        functools.partial(_peer_experts_body, keys=keys, heads=heads),
        out_shape=jax.ShapeDtypeStruct((m, d), F32),
        grid=(m // tm, n_exp // te),
        in_specs=[pl.BlockSpec((tm, d), lambda i, e: (i, 0), pipeline_mode=once),
                  pl.BlockSpec((2 * heads * keys, tm), lambda i, e: (0, i), pipeline_mode=once),
                  pl.BlockSpec((heads, PEER_STAT_ROWS, tm), lambda i, e: (0, 0, i), pipeline_mode=once),
                  tab, tab],
        out_specs=pl.BlockSpec((tm, d), lambda i, e: (i, 0)),
        compiler_params=_params(("parallel", "arbitrary"), 56),
        name="peer_experts",
    )(hn, scores_t, stats, u_tab, v_tab)


def _peer_ffn(hn, wq, subkeys, u_tab, v_tab):
    terms = [(hn, wq, 0)]
    qp = _matmul(terms, wq.shape[1], BF16, _epi_identity, tiles=_matmul_tiles(terms, wq.shape[1], BF16, 0),
                 name="peer_query_proj")
    scores_t, stats = _peer_topk(qp, subkeys)
    return _peer_experts(hn, scores_t, stats, u_tab, v_tab)


def _rope_tables(pos):
    half = LANES // 2
    freqs = ROPE_THETA ** (-jnp.arange(half, dtype=F32) / half)
    ang = pos.astype(F32)[:, None] * freqs[None, :]
    cos, sin = jnp.cos(ang), jnp.sin(ang)
    return jnp.concatenate([cos, cos], axis=1), jnp.concatenate([-sin, sin], axis=1)


def _trunk_layer(x, p_emb, pos, attend, pool, w):
    m, d = x.shape
    attn_w = w["w_up_attn"].shape[0]
    pool_w = w["w_up_pool"].shape[0]
    if pos.shape[0] % LANES != 0:
        pos = jnp.tile(pos, m // pos.shape[0])
    n_pos = pos.shape[0]
    h = _rmsnorm(x, w["norm_mix"])
    cos, sin = _rope_tables(pos)

    def project(cols, width, out_dtype, epilogue=_epi_identity, extras=None, name=None):
        terms = [(h, w["w_in"], cols)]
        tm, tn = tiles = _matmul_tiles(terms, width, out_dtype, 0)
        return _matmul(terms, width, out_dtype, epilogue, extras(tm) if extras else (), tiles=tiles, name=name)

    def rope_extras(gain):
        def build(tm):
            assert n_pos % tm == 0
            pos_blocks = n_pos // tm
            return [(gain.reshape(1, LANES), (1, LANES), lambda i, j: (0, 0)),
                    (cos, (tm, LANES), lambda i, j: (i % pos_blocks, 0)),
                    (sin, (tm, LANES), lambda i, j: (i % pos_blocks, 0))]
        return build

    q = project(0, attn_w, BF16, _epi_qk_norm_rope, rope_extras(w["q_norm"]), "in_proj_q")
    k = project(attn_w, attn_w, F32, _epi_qk_norm_rope, rope_extras(w["k_norm"]), "in_proj_k")
    v = project(2 * attn_w, attn_w, F32, name="in_proj_v")
    u = project(3 * attn_w, pool_w, F32, name="in_proj_u")

    y_attn = attend(q, k, v)
    y_pool = _pool_proj(pool(u), w["pool_w"], w["pool_scale"])

    bias = w["b_gate"].reshape(1, 2 * d)
    terms = [(h, w["w_gate"], 0), (y_attn, w["w_up_attn"], 0)]
    tm, tn = tiles = _matmul_tiles(terms, d, F32, 4)
    gated = _matmul(terms, d, F32, _epi_gate_mul, [(bias, (1, tn), lambda i, j: (0, j))],
                    tiles=tiles, name="gate_attn")
    terms = [(h, w["w_gate"], d), (y_pool, w["w_up_pool"], 0)]
    merged = _matmul(terms, d, BF16, _epi_gate_mul_add,
                     [(bias, (1, tn), lambda i, j: (0, j + d // tn)), (gated, (tm, tn), lambda i, j: (i, j))],
                     tiles=tiles, name="gate_pool_merge")
    terms = [(merged, w["w_out"], 0)]
    tm, tn = tiles = _matmul_tiles(terms, d, F32, 4)
    x = _matmul(terms, d, F32, _epi_residual, [(x, (tm, tn), lambda i, j: (i, j))], tiles=tiles, name="out_proj")

    ffn = _peer_ffn(_rmsnorm(x, w["norm_ffn"]), w["peer_wq"], w["peer_subkeys"], w["peer_u"], w["peer_v"])
    x, xn = _add_rmsnorm(x, ffn, w["ple_norm"])
    terms = [(xn, w["ple_gate"], 0), (p_emb, w["ple_proj"], 0)]
    tm, tn = tiles = _matmul_tiles(terms, d, F32, 4)
    x = _matmul(terms, d, F32, _epi_ple, [(x, (tm, tn), lambda i, j: (i, j))], tiles=tiles, name="ple")
    return x, k, v, u


_MATMUL_WEIGHTS = ("w_in", "w_up_attn", "w_up_pool", "w_gate", "w_out", "peer_wq", "peer_subkeys",
                   "peer_u", "peer_v", "ple_proj", "ple_gate")


def kernel(x_prompt, x_sample, cache_k, cache_v, state_pool, page_table, p_prompt, p_sample, norm_mix, w_in,
           q_norm, k_norm, pool_w, pool_scale, w_up_attn, w_up_pool, w_gate, b_gate, w_out, norm_ffn, peer_wq,
           peer_subkeys, peer_u, peer_v, ple_proj, ple_norm, ple_gate):
    batch, seq, d = x_prompt.shape
    db, dec_seq, _ = x_sample.shape
    depth = norm_mix.shape[0]
    heads, hd = cache_k.shape[3], cache_k.shape[4]
    past = page_table.shape[1] * cache_k.shape[2]
    assert dec_seq == 1, "one new token per sample sequence"
    weights = dict(norm_mix=norm_mix, w_in=w_in, q_norm=q_norm, k_norm=k_norm, pool_w=pool_w,
                   pool_scale=pool_scale, w_up_attn=w_up_attn, w_up_pool=w_up_pool, w_gate=w_gate, b_gate=b_gate,
                   w_out=w_out, norm_ffn=norm_ffn, peer_wq=peer_wq, peer_subkeys=peer_subkeys, peer_u=peer_u,
                   peer_v=peer_v, ple_proj=ple_proj, ple_norm=ple_norm, ple_gate=ple_gate)
    pos_p = jnp.arange(seq, dtype=I32)
    pos_s = past + jnp.arange(dec_seq, dtype=I32)
    yp = x_prompt.reshape(batch * seq, d)
    ys = x_sample.reshape(db * dec_seq, d)
    outs = [[] for _ in range(6)]
    for l in range(depth):
        w = {name: (t[l].astype(BF16) if name in _MATMUL_WEIGHTS else t[l]) for name, t in weights.items()}

        def attend_p(q, k, v):
            return _moba_prompt(q, k, v, batch, seq, heads)

        def attend_s(q, k, v):
            y = _moba_sample(q.reshape(db, heads, hd), k.reshape(db, heads, hd), v.reshape(db, heads, hd),
                             cache_k, cache_v, l, page_table)
            return y.reshape(db, heads * hd)

        yp, kp, vp, up = _trunk_layer(yp, p_prompt[l].reshape(batch * seq, -1), pos_p, attend_p,
                                      lambda u: _pool_prompt(u, batch, seq), w)
        ys, ks, vs, us = _trunk_layer(ys, p_sample[l].reshape(db * dec_seq, -1), pos_s, attend_s,
                                      lambda u, l=l: _pool_sample(u, jnp.swapaxes(state_pool[l], 0, 1)), w)
        n_state = state_pool.shape[2]
        pool_p = up.reshape(batch, seq, -1)[:, seq - n_state:]
        pool_s = jnp.concatenate([state_pool[l], us.reshape(db, dec_seq, -1)], axis=1)[:, dec_seq:]
        for dst, val in zip(outs, (kp.reshape(batch, seq, heads, hd), vp.reshape(batch, seq, heads, hd), pool_p,
                                   ks.reshape(db, dec_seq, heads, hd), vs.reshape(db, dec_seq, heads, hd),
                                   pool_s)):
            dst.append(val)
    return (yp.reshape(batch, seq, d), ys.reshape(db, dec_seq, d)) + tuple(jnp.stack(o) for o in outs)
```
